```python
import math
import jax, jax.numpy as jnp
from jax import lax
import numpy as np

D_MODEL = 2048
BATCH = 4
SEQ = 2048
DEPTH = 2
DEC_BATCH = 128
DEC_SEQ = 4
PAST_LEN = 16384
PAGE_SIZE = 128

N_EVEN = (DEPTH + 1) // 2
N_ODD = DEPTH // 2
W_A = D_MODEL // 2
S5_GC = 16
S5_GROUPS = W_A // S5_GC
S5_P = 64
W_B = D_MODEL // 2
SGU_HEADS = 8
SGU_DH = W_B // SGU_HEADS
SGU_CHUNK = 128
W_C = D_MODEL // 2
RG_BLOCKS = 8
RG_BS = W_C // RG_BLOCKS
CONV_W = 4
RG_C = 8.0
W_D = D_MODEL // 2
HG_DK = 128
HG_HEADS = W_D // HG_DK
HG_DV = W_D // HG_HEADS
HG_CHUNK = 64
MOE_GROUPS = 4
MOE_PER_GROUP = 8
N_EXPERTS = MOE_GROUPS * MOE_PER_GROUP
MOE_TOPK = 2
D_EXPERT = D_MODEL // 4
EPS = 1e-6
F32 = jnp.float32

kernel_name = 'hybrid_s5_sgu_rglru_hgrn2_hmoe_step'


def _rmsnorm(x, g):
    x32 = x.astype(F32)
    return x32 * lax.rsqrt(jnp.mean(x32 * x32, axis=-1, keepdims=True) + EPS) * g.astype(F32)


def _layernorm(x, g, b):
    x32 = x.astype(F32)
    mu = jnp.mean(x32, axis=-1, keepdims=True)
    xc = x32 - mu
    var = jnp.mean(xc * xc, axis=-1, keepdims=True)
    return xc * lax.rsqrt(var + EPS) * g.astype(F32) + b.astype(F32)


def _lin_scan(a, b, h0):
    def comb(l, r):
        al, bl = l
        ar, br = r
        return al * ar, ar * bl + br
    a_cum, b_cum = lax.associative_scan(comb, (a, b), axis=1)
    return a_cum * h0[:, None] + b_cum


def _clin_scan(ar, ai, br, bi, h0r, h0i):
    def comb(l, r):
        a1r, a1i, b1r, b1i = l
        a2r, a2i, b2r, b2i = r
        return (a2r * a1r - a2i * a1i, a2r * a1i + a2i * a1r,
                a2r * b1r - a2i * b1i + b2r, a2r * b1i + a2i * b1r + b2i)
    Ar, Ai, Br, Bi = lax.associative_scan(comb, (ar, ai, br, bi), axis=1)
    h0r = h0r[:, None]
    h0i = h0i[:, None]
    return Ar * h0r - Ai * h0i + Br, Ar * h0i + Ai * h0r + Bi


def _s5(u, h0r, h0i, p, e):
    bsz, L, _ = u.shape
    lr = p['s5_lambda_re'][e].astype(F32)
    lim = p['s5_lambda_im'][e].astype(F32)
    dt = jnp.exp(p['s5_log_dt'][e].astype(F32))[:, None]
    mag = jnp.exp(lr * dt)
    abr = mag * jnp.cos(lim * dt)
    abi = mag * jnp.sin(lim * dt)
    den = lr * lr + lim * lim
    cr = ((abr - 1.0) * lr + abi * lim) / den
    ci = (abi * lr - (abr - 1.0) * lim) / den
    bre = p['s5_b_re'][e].astype(F32)
    bim = p['s5_b_im'][e].astype(F32)
    bbr = cr[..., None] * bre - ci[..., None] * bim
    bbi = cr[..., None] * bim + ci[..., None] * bre
    ug = u.astype(F32).reshape(bsz, L, S5_GROUPS, S5_GC)
    bur = jnp.einsum('gpc,blgc->blgp', bbr, ug)
    bui = jnp.einsum('gpc,blgc->blgp', bbi, ug)
    ar = jnp.broadcast_to(abr, bur.shape)
    ai = jnp.broadcast_to(abi, bur.shape)
    hr, hi = _clin_scan(ar, ai, bur, bui, h0r.astype(F32), h0i.astype(F32))
    y = (jnp.einsum('gcp,blgp->blgc', p['s5_c_re'][e].astype(F32), hr)
         - jnp.einsum('gcp,blgp->blgc', p['s5_c_im'][e].astype(F32), hi)
         + p['s5_d'][e].astype(F32) * ug)
    y = jax.nn.gelu(y.reshape(bsz, L, W_A))
    y = y * jax.nn.sigmoid(y @ p['s5_glu_w'][e].astype(F32) + p['s5_glu_b'][e].astype(F32))
    return y, hr[:, -1], hi[:, -1]


def _sgu(u, v, p, e):
    bsz, L, _ = v.shape
    vn = _layernorm(v, p['sgu_norm_g'][e], p['sgu_norm_b'][e])
    n_chunks = -(-L // SGU_CHUNK)
    pad = n_chunks * SGU_CHUNK - L
    vp = jnp.pad(vn, ((0, 0), (0, pad), (0, 0))).reshape(bsz, n_chunks, SGU_CHUNK, SGU_HEADS, SGU_DH)
    mask = jnp.tril(jnp.ones((SGU_CHUNK, SGU_CHUNK), dtype=bool))
    wm = jnp.where(mask[None], p['sgu_w'][e].astype(F32), 0.0)
    mix = jnp.einsum('hts,bnshd->bnthd', wm, vp) + p['sgu_b'][e].astype(F32).T[None, None, :, :, None]
    mix = mix.reshape(bsz, n_chunks * SGU_CHUNK, W_B)[:, :L]
    return u.astype(F32) * mix, vn


def _mixer_ab(h, s5r0, s5i0, p, e):
    z = h @ p['w_in_ab'][e]
    u_a = z[..., :W_A]
    u_b = jax.nn.gelu(z[..., W_A:W_A + W_B])
    v_b = jax.nn.gelu(z[..., W_A + W_B:])
    y_a, s5r, s5i = _s5(u_a, s5r0, s5i0, p, e)
    y_b, v_rows = _sgu(u_b, v_b, p, e)
    out = jnp.concatenate([y_a, y_b], axis=-1) @ p['w_out_ab'][e].astype(F32)
    return out, s5r, s5i, v_rows


def _causal_conv(x, buf, w, b):
    L = x.shape[1]
    xp = jnp.concatenate([buf.astype(F32), x.astype(F32)], axis=1)
    y = b.astype(F32)
    for j in range(CONV_W):
        y = y + w[j].astype(F32) * xp[:, j:j + L]
    return y, xp[:, -(CONV_W - 1):]


def _rglru(xc, h0, p, o):
    bsz, L, _ = xc.shape
    xb = xc.reshape(bsz, L, RG_BLOCKS, RG_BS)
    r = jax.nn.sigmoid(jnp.einsum('blhi,hij->blhj', xb, p['rglru_wa'][o].astype(F32)).reshape(bsz, L, W_C)
                       + p['rglru_ba'][o].astype(F32))
    i = jax.nn.sigmoid(jnp.einsum('blhi,hij->blhj', xb, p['rglru_wx'][o].astype(F32)).reshape(bsz, L, W_C)
                       + p['rglru_bx'][o].astype(F32))
    log_a = -RG_C * r * jax.nn.softplus(-p['rglru_lambda'][o].astype(F32))
    a = jnp.exp(log_a)
    mult = jnp.sqrt(-jnp.expm1(2.0 * log_a))
    hseq = _lin_scan(a, mult * i * xc, h0.astype(F32))
    return hseq, hseq[:, -1]


def _hgrn2(q, logf, k, v, s0):
    bsz, L, H, DK = q.shape
    DV = v.shape[-1]
    C = HG_CHUNK if L % HG_CHUNK == 0 else L
    n = L // C

    def to_chunks(t):
        return jnp.moveaxis(t.reshape(bsz, n, C, H, t.shape[-1]), 1, 0)

    mask = jnp.tril(jnp.ones((C, C), dtype=bool))[None, :, :, None, None]

    def step(S, inp):
        qc, lfc, kc, vc = inp
        g = jnp.cumsum(lfc, axis=1)
        o_inter = jnp.einsum('bthk,bhkv->bthv', qc * jnp.exp(g), S)
        decay = jnp.exp(jnp.where(mask, g[:, :, None] - g[:, None], -jnp.inf))
        att = jnp.einsum('bthk,bshk,btshk->bths', qc, kc, decay)
        o = o_inter + jnp.einsum('bths,bshv->bthv', att, vc)
        gl = g[:, -1]
        S_new = jnp.exp(gl)[..., None] * S + jnp.einsum('bshk,bshv->bhkv', kc * jnp.exp(gl[:, None] - g), vc)
        return S_new, o

    S, o = lax.scan(step, s0.astype(F32), (to_chunks(q), to_chunks(logf), to_chunks(k), to_chunks(v)))
    o = jnp.moveaxis(o, 0, 1).reshape(bsz, L, H, DV)
    return o, S


def _mixer_cd(h, conv0, rh0, s0, p, o, layer):
    bsz, L, _ = h.shape
    z = (h @ p['w_in_cd'][o]).astype(F32)
    off = np.cumsum([0, W_C, W_C, W_D, W_D, W_D, W_D])
    xr, xg, q, f, iv, g = [z[..., int(off[j]):int(off[j + 1])] for j in range(6)]
    xc, conv_new = _causal_conv(xr, conv0, p['rglru_conv_w'][o], p['rglru_conv_b'][o])
    hseq, h_last = _rglru(xc, rh0, p, o)
    y_c = hseq * jax.nn.gelu(xg)
    sm = jax.nn.softmax(p['hgrn_lb'].astype(F32), axis=0)
    cs = jnp.cumsum(sm, axis=0)
    lb = cs[layer] - cs[0]
    fg = lb + (1.0 - lb) * jax.nn.sigmoid(f)
    logf = jnp.log(fg)
    kk = 1.0 - fg
    hs = lambda t, d: t.reshape(bsz, L, HG_HEADS, d)
    od, S_new = _hgrn2(hs(q, HG_DK), hs(logf, HG_DK), hs(kk, HG_DK), hs(iv, HG_DV), s0)
    od = od * lax.rsqrt(jnp.mean(od * od, axis=-1, keepdims=True) + EPS)
    od = od.reshape(bsz, L, W_D) * p['hgrn_norm_g'][o].astype(F32) * jax.nn.silu(g)
    out = jnp.concatenate([y_c, od], axis=-1) @ p['w_out_cd'][o].astype(F32)
    return out, h_last, conv_new, S_new


def _moe(h, p, l):
    bsz, L, D = h.shape
    t = h.reshape(-1, D).astype(F32)
    pg = jax.nn.softmax(t @ p['moe_w_grp'][l].astype(F32) + p['moe_b_grp'][l].astype(F32), axis=-1)
    gw, gi = lax.top_k(pg, 1)
    le = (t @ p['moe_w_exp'][l].astype(F32) + p['moe_b_exp'][l].astype(F32)).reshape(-1, MOE_GROUPS, MOE_PER_GROUP)
    le = jnp.take_along_axis(le, gi[:, :, None], axis=1)[:, 0]
    ew, ei = lax.top_k(jax.nn.softmax(le, axis=-1), MOE_TOPK)
    ew = ew / jnp.sum(ew, axis=-1, keepdims=True)
    eidx = gi * MOE_PER_GROUP + ei
    comb = gw * ew
    gate = jnp.sum(jax.nn.one_hot(eidx, N_EXPERTS, dtype=F32) * comb[..., None], axis=1)
    wg = p['moe_w_gate'][l].astype(F32)
    wu = p['moe_w_up'][l].astype(F32)
    wd = p['moe_w_down'][l].astype(F32)
    out = jnp.zeros_like(t)
    for grp in range(MOE_GROUPS):
        sl = slice(grp * MOE_PER_GROUP, (grp + 1) * MOE_PER_GROUP)
        hid = jax.nn.silu(jnp.einsum('td,edf->tef', t, wg[sl])) * jnp.einsum('td,edf->tef', t, wu[sl])
        out = out + jnp.einsum('tef,efd->td', hid * gate[:, sl, None], wd[sl])
    return out.reshape(bsz, L, D)


def _trunk(x, c, s5r, s5i, rgh, rgc, hgs, p):
    sc = jax.nn.silu(c.astype(F32))
    x = x.astype(F32)
    o_s5r, o_s5i, o_v, o_rgh, o_rgc, o_hgs = [], [], [], [], [], []
    for l in range(DEPTH):
        mod = sc @ p['ada_w'][l].astype(F32) + p['ada_b'][l].astype(F32)
        sh_m, sc_m, g_m, sh_f, sc_f, g_f = [m[:, None] for m in jnp.split(mod, 6, axis=-1)]
        hmix = _rmsnorm(x, p['norm_mix_g'][l]) * (1.0 + sc_m) + sh_m
        if l % 2 == 0:
            e = l // 2
            out, r_, i_, v_ = _mixer_ab(hmix, s5r[e], s5i[e], p, e)
            o_s5r.append(r_)
            o_s5i.append(i_)
            o_v.append(v_)
        else:
            o = l // 2
            out, hl, cb, S = _mixer_cd(hmix, rgc[o], rgh[o], hgs[o], p, o, l)
            o_rgh.append(hl)
            o_rgc.append(cb)
            o_hgs.append(S)
        x = x + g_m * out
        hf = _rmsnorm(x, p['norm_ffn_g'][l]) * (1.0 + sc_f) + sh_f
        x = x + g_f * _moe(hf, p, l)
    y = _rmsnorm(x, p['norm_final_g'])
    return (y, jnp.stack(o_s5r), jnp.stack(o_s5i), jnp.stack(o_v),
            jnp.stack(o_rgh), jnp.stack(o_rgc), jnp.stack(o_hgs))


def setup_inputs(seed: int = 0) -> dict:
    key = jax.random.key(seed)
    ks = iter(jax.random.split(key, 64))
    D = D_MODEL

    def nrm(shape, scale):
        return jax.random.normal(next(ks), shape, F32) * scale

    def gain(shape):
        return 1.0 + 0.05 * jax.random.normal(next(ks), shape, F32)

    n_in_ab = W_A + 2 * W_B
    n_in_cd = 2 * W_C + 4 * W_D
    s5_lambda_re = -0.5 * (1.0 + 0.1 * jax.random.uniform(next(ks), (N_EVEN, S5_GROUPS, S5_P), F32))
    s5_lambda_im = (jnp.broadcast_to(jnp.arange(S5_P, dtype=F32) * math.pi, (N_EVEN, S5_GROUPS, S5_P))
                    + nrm((N_EVEN, S5_GROUPS, S5_P), 0.01))
    s5_log_dt = jax.random.uniform(next(ks), (N_EVEN, S5_GROUPS), F32, math.log(1e-3), math.log(1e-1))
    a_c = jax.random.uniform(next(ks), (N_ODD, W_C), F32, 0.9, 0.999)
    s_root = a_c ** (1.0 / RG_C)
    rglru_lambda = jnp.log(s_root) - jnp.log1p(-s_root)
    return {
        'x_prompt': nrm((BATCH, SEQ, D), 1.0),
        'x_sample': nrm((DEC_BATCH, DEC_SEQ, D), 1.0),
        'c_prompt': nrm((BATCH, D), 1.0),
        'c_sample': nrm((DEC_BATCH, D), 1.0),
        'state_s5_re': nrm((N_EVEN, DEC_BATCH, S5_GROUPS, S5_P), 0.1),
        'state_s5_im': nrm((N_EVEN, DEC_BATCH, S5_GROUPS, S5_P), 0.1),
        'state_rglru_h': nrm((N_ODD, DEC_BATCH, W_C), 0.5),
        'state_rglru_conv': nrm((N_ODD, DEC_BATCH, CONV_W - 1, W_C), 1.0),
        'state_hgrn2': nrm((N_ODD, DEC_BATCH, HG_HEADS, HG_DK, HG_DV), 0.3),
        'ada_w': nrm((DEPTH, D, 6 * D), 0.5 * D ** -0.5),
        'ada_b': nrm((DEPTH, 6 * D), 0.02),
        'norm_mix_g': gain((DEPTH, D)),
        'norm_ffn_g': gain((DEPTH, D)),
        'norm_final_g': gain((D,)),
        'w_in_ab': nrm((N_EVEN, D, n_in_ab), D ** -0.5),
        'w_out_ab': nrm((N_EVEN, W_A + W_B, D), (W_A + W_B) ** -0.5),
        's5_lambda_re': s5_lambda_re,
        's5_lambda_im': s5_lambda_im,
        's5_log_dt': s5_log_dt,
        's5_b_re': nrm((N_EVEN, S5_GROUPS, S5_P, S5_GC), (2 * S5_GC) ** -0.5),
        's5_b_im': nrm((N_EVEN, S5_GROUPS, S5_P, S5_GC), (2 * S5_GC) ** -0.5),
        's5_c_re': nrm((N_EVEN, S5_GROUPS, S5_GC, S5_P), S5_P ** -0.5),
        's5_c_im': nrm((N_EVEN, S5_GROUPS, S5_GC, S5_P), S5_P ** -0.5),
        's5_d': nrm((N_EVEN, S5_GROUPS, S5_GC), 1.0),
        's5_glu_w': nrm((N_EVEN, W_A, W_A), W_A ** -0.5),
        's5_glu_b': nrm((N_EVEN, W_A), 0.02),
        'sgu_norm_g': gain((N_EVEN, W_B)),
        'sgu_norm_b': nrm((N_EVEN, W_B), 0.02),
        'sgu_w': nrm((N_EVEN, SGU_HEADS, SGU_CHUNK, SGU_CHUNK), SGU_CHUNK ** -0.5),
        'sgu_b': gain((N_EVEN, SGU_HEADS, SGU_CHUNK)),
        'w_in_cd': nrm((N_ODD, D, n_in_cd), D ** -0.5),
        'w_out_cd': nrm((N_ODD, W_C + W_D, D), (W_C + W_D) ** -0.5),
        'rglru_conv_w': nrm((N_ODD, CONV_W, W_C), CONV_W ** -0.5),
        'rglru_conv_b': nrm((N_ODD, W_C), 0.02),
        'rglru_wa': nrm((N_ODD, RG_BLOCKS, RG_BS, RG_BS), RG_BS ** -0.5),
        'rglru_ba': nrm((N_ODD, W_C), 0.02),
        'rglru_wx': nrm((N_ODD, RG_BLOCKS, RG_BS, RG_BS), RG_BS ** -0.5),
        'rglru_bx': nrm((N_ODD, W_C), 0.02),
        'rglru_lambda': rglru_lambda,
        'hgrn_lb': nrm((DEPTH, W_D), 1.0),
        'hgrn_norm_g': gain((N_ODD, W_D)),
        'moe_w_grp': nrm((DEPTH, D, MOE_GROUPS), D ** -0.5),
        'moe_b_grp': nrm((DEPTH, MOE_GROUPS), 0.01),
        'moe_w_exp': nrm((DEPTH, D, N_EXPERTS), D ** -0.5),
        'moe_b_exp': nrm((DEPTH, N_EXPERTS), 0.01),
        'moe_w_gate': nrm((DEPTH, N_EXPERTS, D, D_EXPERT), D ** -0.5),
        'moe_w_up': nrm((DEPTH, N_EXPERTS, D, D_EXPERT), D ** -0.5),
        'moe_w_down': nrm((DEPTH, N_EXPERTS, D_EXPERT, D), D_EXPERT ** -0.5),
    }


def reference(x_prompt, x_sample, c_prompt, c_sample, state_s5_re, state_s5_im, state_rglru_h,
              state_rglru_conv, state_hgrn2, ada_w, ada_b, norm_mix_g, norm_ffn_g, norm_final_g,
              w_in_ab, w_out_ab, s5_lambda_re, s5_lambda_im, s5_log_dt, s5_b_re, s5_b_im, s5_c_re,
              s5_c_im, s5_d, s5_glu_w, s5_glu_b, sgu_norm_g, sgu_norm_b, sgu_w, sgu_b, w_in_cd, w_out_cd,
              rglru_conv_w, rglru_conv_b, rglru_wa, rglru_ba, rglru_wx, rglru_bx, rglru_lambda, hgrn_lb,
              hgrn_norm_g, moe_w_grp, moe_b_grp, moe_w_exp, moe_b_exp, moe_w_gate, moe_w_up, moe_w_down):
    p = dict(ada_w=ada_w, ada_b=ada_b, norm_mix_g=norm_mix_g, norm_ffn_g=norm_ffn_g,
             norm_final_g=norm_final_g, w_in_ab=w_in_ab, w_out_ab=w_out_ab, s5_lambda_re=s5_lambda_re,
             s5_lambda_im=s5_lambda_im, s5_log_dt=s5_log_dt, s5_b_re=s5_b_re, s5_b_im=s5_b_im,
             s5_c_re=s5_c_re, s5_c_im=s5_c_im, s5_d=s5_d, s5_glu_w=s5_glu_w, s5_glu_b=s5_glu_b,
             sgu_norm_g=sgu_norm_g, sgu_norm_b=sgu_norm_b, sgu_w=sgu_w, sgu_b=sgu_b, w_in_cd=w_in_cd,
             w_out_cd=w_out_cd, rglru_conv_w=rglru_conv_w, rglru_conv_b=rglru_conv_b, rglru_wa=rglru_wa,
             rglru_ba=rglru_ba, rglru_wx=rglru_wx, rglru_bx=rglru_bx, rglru_lambda=rglru_lambda,
             hgrn_lb=hgrn_lb, hgrn_norm_g=hgrn_norm_g, moe_w_grp=moe_w_grp, moe_b_grp=moe_b_grp,
             moe_w_exp=moe_w_exp, moe_b_exp=moe_b_exp, moe_w_gate=moe_w_gate, moe_w_up=moe_w_up,
             moe_w_down=moe_w_down)
    bp = x_prompt.shape[0]
    z_s5 = jnp.zeros((N_EVEN, bp, S5_GROUPS, S5_P), F32)
    z_rgh = jnp.zeros((N_ODD, bp, W_C), F32)
    z_rgc = jnp.zeros((N_ODD, bp, CONV_W - 1, W_C), F32)
    z_hg = jnp.zeros((N_ODD, bp, HG_HEADS, HG_DK, HG_DV), F32)
    y_prompt, s5_re_p, s5_im_p, _, rgh_p, rgc_p, hg_p = _trunk(
        x_prompt, c_prompt, z_s5, z_s5, z_rgh, z_rgc, z_hg, p)
    y_sample, s5_re_s, s5_im_s, sgu_v_s, rgh_s, rgc_s, hg_s = _trunk(
        x_sample, c_sample, state_s5_re, state_s5_im, state_rglru_h, state_rglru_conv, state_hgrn2, p)
    return (y_prompt, y_sample, s5_re_p, s5_im_p, s5_re_s, s5_im_s, sgu_v_s,
            rgh_p, rgh_s, rgc_p, rgc_s, hg_p, hg_s)
```

```python
import functools

import numpy as np
import jax
import jax.numpy as jnp
from jax import lax
from jax.experimental import pallas as pl
from jax.experimental.pallas import tpu as pltpu

F32 = jnp.float32
BF16 = jnp.bfloat16
HIGHEST = lax.Precision.HIGHEST

D = 2048
B_P, L_P = 4, 2048
B_S, L_S = 128, 4
T_P = B_P * L_P
T_S = B_S * L_S
T = T_P + T_S
W = D // 2
EPS = 1e-6
S5_G, S5_P, S5_GC = 64, 64, 16
HEADS = 8
HD = 128
CONV_W = 4
RG_C = 8.0
N_GRP, PER_GRP, N_EXP = 4, 8, 32
D_EXP = D // 4

V7X_VMEM_BYTES = 64 * 2 ** 20
VMEM_LIMIT = (V7X_VMEM_BYTES * 3) // 4

TM = 256
TMM = 512
TR = 256
R_TOT = 2 * T + N_EXP * TR
N_RT = R_TOT // TR


def _cp(sem, vmem=VMEM_LIMIT):
    return pltpu.CompilerParams(dimension_semantics=sem, vmem_limit_bytes=vmem)


def _dot(a, b):
    return jnp.dot(a, b, preferred_element_type=F32)


def _dot_exact(a, b):
    return jnp.dot(a, b, preferred_element_type=F32, precision=HIGHEST)


def _dot_tn(a, b, precision=None):
    return lax.dot_general(a, b, (((0,), (0,)), ((), ())), preferred_element_type=F32, precision=precision)


def _dot_nt(a, b):
    return lax.dot_general(a, b, (((1,), (1,)), ((), ())), preferred_element_type=F32)


def _split_bf16(x):
    hi = x.astype(BF16)
    lo = (x - hi.astype(F32)).astype(BF16)
    return hi, lo


def _mod_specs(tm, chunk, width=D, col=lambda *g: 0, tile=lambda *g: g[0]):
    npt = T_P // tm
    per_chunk = D // width

    def p_map(*g):
        return (jnp.minimum(tile(*g) * tm // L_P, B_P - 1), 0, chunk * per_chunk + col(*g))

    def s_map(*g):
        return (jnp.maximum(tile(*g) - npt, 0), chunk * per_chunk + col(*g))

    return [pl.BlockSpec((None, 1, width), p_map), pl.BlockSpec((tm, width), s_map)]


def _mod_pick(is_prompt, p_ref, s_ref):
    return jnp.where(is_prompt, p_ref[...], s_ref[...])


def _adaln_body(c_ref, w_ref, b_ref, o_ref):
    c = c_ref[...]
    sc = c * jax.nn.sigmoid(c)
    a_hi, a_lo = _split_bf16(sc)
    w_hi, w_lo = _split_bf16(w_ref[...])
    o_ref[...] = _dot(a_hi, w_hi) + _dot(a_hi, w_lo) + _dot(a_lo, w_hi) + b_ref[...]


def _adaln(c_all, ada_w, ada_b):
    nl = ada_w.shape[0]
    rows = c_all.shape[0]
    tn = 1024
    return pl.pallas_call(
        _adaln_body,
        grid=(nl, 6 * D // tn),
        in_specs=[pl.BlockSpec((rows, D), lambda l, j: (0, 0)),
                  pl.BlockSpec((None, D, tn), lambda l, j: (l, 0, j)),
                  pl.BlockSpec((None, 1, tn), lambda l, j: (l, 0, j))],
        out_specs=pl.BlockSpec((None, rows, tn), lambda l, j: (l, 0, j)),
        out_shape=jax.ShapeDtypeStruct((nl, rows, 6 * D), F32),
        compiler_params=_cp(("arbitrary", "arbitrary")),
    )(c_all, ada_w, ada_b.reshape(nl, 1, 6 * D))


def _normmod_body(x_ref, g_ref, scp_ref, scs_ref, shp_ref, shs_ref, o_ref):
    is_p = pl.program_id(0) < T_P // TM
    x = x_ref[...]
    y = x * lax.rsqrt(jnp.mean(x * x, axis=-1, keepdims=True) + EPS) * g_ref[...]
    y = y * (1.0 + _mod_pick(is_p, scp_ref, scs_ref)) + _mod_pick(is_p, shp_ref, shs_ref)
    o_ref[...] = y.astype(o_ref.dtype)


def _normmod(x, g, mod_p, mod_s, sc_chunk, sh_chunk, dtype):
    return pl.pallas_call(
        _normmod_body,
        grid=(T // TM,),
        in_specs=[pl.BlockSpec((TM, D), lambda i: (i, 0)),
                  pl.BlockSpec((1, D), lambda i: (0, 0))]
                 + _mod_specs(TM, sc_chunk) + _mod_specs(TM, sh_chunk),
        out_specs=pl.BlockSpec((TM, D), lambda i: (i, 0)),
        out_shape=jax.ShapeDtypeStruct((T, D), dtype),
        compiler_params=_cp(("arbitrary",)),
    )(x, g.reshape(1, D), mod_p, mod_s, mod_p, mod_s)


def _finalnorm_body(x_ref, g_ref, o_ref):
    x = x_ref[...]
    o_ref[...] = x * lax.rsqrt(jnp.mean(x * x, axis=-1, keepdims=True) + EPS) * g_ref[...]


def _finalnorm(x, g):
    return pl.pallas_call(
        _finalnorm_body,
        grid=(T // TM,),
        in_specs=[pl.BlockSpec((TM, D), lambda i: (i, 0)), pl.BlockSpec((1, D), lambda i: (0, 0))],
        out_specs=pl.BlockSpec((TM, D), lambda i: (i, 0)),
        out_shape=jax.ShapeDtypeStruct((T, D), F32),
        compiler_params=_cp(("arbitrary",)),
    )(x, g.reshape(1, D))


def _mm_in_body(a_ref, w_ref, o_ref, wb):
    @pl.when(pl.program_id(1) == 0)
    def _():
        wb[...] = w_ref[...].astype(BF16)

    o_ref[...] = _dot(a_ref[...], wb[...])


def _mm_in(a, w3, layer):
    n = w3.shape[-1]
    tn = 1024
    return pl.pallas_call(
        _mm_in_body,
        grid=(n // tn, T // TMM),
        in_specs=[pl.BlockSpec((TMM, D), lambda j, i: (i, 0)),
                  pl.BlockSpec((None, D, tn), lambda j, i: (layer, 0, j))],
        out_specs=pl.BlockSpec((TMM, tn), lambda j, i: (i, j)),
        out_shape=jax.ShapeDtypeStruct((T, n), F32),
        scratch_shapes=[pltpu.VMEM((D, tn), BF16)],
        compiler_params=_cp(("arbitrary", "arbitrary")),
    )(a, w3)


def _mm_out_body(a0_ref, a1_ref, w_ref, x_ref, gp_ref, gs_ref, o_ref, wb):
    i = pl.program_id(1)

    @pl.when(i == 0)
    def _():
        wb[...] = w_ref[...].astype(BF16)

    out = _dot(a0_ref[...], wb[0:W, :]) + _dot(a1_ref[...], wb[W:D, :])
    gate = _mod_pick(i < T_P // TMM, gp_ref, gs_ref)
    o_ref[...] = x_ref[...] + gate * out


def _mm_out(a0, a1, w3, layer, x, mod_p, mod_s, gate_chunk):
    tn = 1024
    return pl.pallas_call(
        _mm_out_body,
        grid=(D // tn, T // TMM),
        in_specs=[pl.BlockSpec((TMM, W), lambda j, i: (i, 0)),
                  pl.BlockSpec((TMM, W), lambda j, i: (i, 0)),
                  pl.BlockSpec((None, D, tn), lambda j, i: (layer, 0, j)),
                  pl.BlockSpec((TMM, tn), lambda j, i: (i, j))]
                 + _mod_specs(TMM, gate_chunk, width=tn, col=lambda j, i: j, tile=lambda j, i: i),
        out_specs=pl.BlockSpec((TMM, tn), lambda j, i: (i, j)),
        out_shape=jax.ShapeDtypeStruct((T, D), F32),
        scratch_shapes=[pltpu.VMEM((D, tn), BF16)],
        compiler_params=_cp(("arbitrary", "arbitrary")),
    )(a0, a1, w3, x, mod_p, mod_s)


def _s5_body(lseg, nseg, u_ref, h0r_ref, h0i_ref, lre_ref, lim_ref, ldt_ref, bre_ref, bim_ref,
             cre_ref, cim_ref, d_ref, gw_ref, gb_ref, ya_ref, hro_ref, hio_ref, *scr):
    sr, si = scr[0:4], scr[4:8]
    car_r, car_i = scr[8], scr[9]
    rows = lseg * nseg

    @pl.when(pl.program_id(1) == 0)
    def _():
        car_r[...] = h0r_ref[...]
        car_i[...] = h0i_ref[...]

    lr = lre_ref[...]
    li = lim_ref[...]
    dt = jnp.exp(ldt_ref[...])
    mag = jnp.exp(lr * dt)
    abr = mag * jnp.cos(li * dt)
    abi = mag * jnp.sin(li * dt)
    den = lr * lr + li * li
    cr = ((abr - 1.0) * lr + abi * li) / den
    ci = (abi * lr - (abr - 1.0) * li) / den

    u = u_ref[...]
    ub = u.astype(BF16)
    for s in range(8):
        us = ub[:, HD * s:HD * (s + 1)]
        pr = _dot(us, bre_ref[s])
        pi = _dot(us, bim_ref[s])
        for q in range(4):
            sr[q][pl.ds(s, rows, stride=8), :] = pr[:, HD * q:HD * (q + 1)]
            si[q][pl.ds(s, rows, stride=8), :] = pi[:, HD * q:HD * (q + 1)]

    lanes = [slice(HD * q, HD * (q + 1)) for q in range(4)]
    a_r = [abr[:, l] for l in lanes]
    a_i = [abi[:, l] for l in lanes]
    c_r = [cr[:, l] for l in lanes]
    c_i = [ci[:, l] for l in lanes]

    def seg(j, carry):
        h0 = tuple(car_r[j, :, l] for l in lanes) + tuple(car_i[j, :, l] for l in lanes)

        def step(t, h):
            base = pl.multiple_of((j * lseg + t) * 8, 8)
            new_r, new_i = [], []
            for q in range(4):
                pr = sr[q][pl.ds(base, 8), :]
                pi = si[q][pl.ds(base, 8), :]
                hr, hi = h[q], h[4 + q]
                nr = a_r[q] * hr - a_i[q] * hi + (c_r[q] * pr - c_i[q] * pi)
                ni = a_r[q] * hi + a_i[q] * hr + (c_r[q] * pi + c_i[q] * pr)
                sr[q][pl.ds(base, 8), :] = nr
                si[q][pl.ds(base, 8), :] = ni
                new_r.append(nr)
                new_i.append(ni)
            return tuple(new_r) + tuple(new_i)

        h = lax.fori_loop(0, lseg, step, h0, unroll=min(lseg, 8))
        for q in range(4):
            car_r[j, :, lanes[q]] = h[q]
            car_i[j, :, lanes[q]] = h[4 + q]
        return carry

    lax.fori_loop(0, nseg, seg, 0)
    hro_ref[...] = car_r[...]
    hio_ref[...] = car_i[...]

    ys = []
    for s in range(8):
        hr_s = jnp.concatenate([sr[q][pl.ds(s, rows, stride=8), :] for q in range(4)], axis=1).astype(BF16)
        hi_s = jnp.concatenate([si[q][pl.ds(s, rows, stride=8), :] for q in range(4)], axis=1).astype(BF16)
        blk = slice(HD * s, HD * (s + 1))
        ys.append(_dot(hr_s, cre_ref[s]) - _dot(hi_s, cim_ref[s]) + d_ref[:, blk] * u[:, blk])
    y = jax.nn.gelu(jnp.concatenate(ys, axis=1))
    gate = jax.nn.sigmoid(_dot(y.astype(BF16), gw_ref[...]) + gb_ref[...])
    ya_ref[...] = (y * gate).astype(BF16)


def _s5(z, row0, nb, lseq, h0r, h0i, prm):
    if lseq >= 256:
        lseg, nseg = 256, 1
        grid = (nb, lseq // lseg)
        row_map = lambda b, c: (row0 // lseg + b * (lseq // lseg) + c, 0)
        out_row_map = lambda b, c: (b * (lseq // lseg) + c, 0)
    else:
        lseg, nseg = lseq, 128 // lseq
        grid = (nb // nseg, 1)
        row_map = lambda b, c: (row0 // 128 + b, 0)
        out_row_map = lambda b, c: (b, 0)
    rows = lseg * nseg
    st_spec = pl.BlockSpec((nseg, 8, 512), lambda b, c: (b, 0, 0))
    full = lambda shape: pl.BlockSpec(shape, lambda b, c: (0,) * len(shape))
    return pl.pallas_call(
        functools.partial(_s5_body, lseg, nseg),
        grid=grid,
        in_specs=[pl.BlockSpec((rows, W), row_map), st_spec, st_spec,
                  full((8, 512)), full((8, 512)), full((8, 512)),
                  full((8, HD, 512)), full((8, HD, 512)), full((8, 512, HD)), full((8, 512, HD)),
                  full((1, W)), full((W, W)), full((1, W))],
        out_specs=[pl.BlockSpec((rows, W), out_row_map), st_spec, st_spec],
        out_shape=[jax.ShapeDtypeStruct((nb * lseq, W), BF16),
                   jax.ShapeDtypeStruct((nb, 8, 512), F32),
                   jax.ShapeDtypeStruct((nb, 8, 512), F32)],
        scratch_shapes=[pltpu.VMEM((rows * 8, HD), F32)] * 8 + [pltpu.VMEM((nseg, 8, 512), F32)] * 2,
        compiler_params=_cp(("arbitrary", "arbitrary")),
    )(z, h0r, h0i, *prm)


def _block_diag(w, transpose):
    if transpose:
        w = jnp.swapaxes(w, 1, 2)
    a, b = w.shape[1], w.shape[2]
    w4 = w.reshape(8, 8, a, b)
    eye = jnp.eye(8, dtype=bool)[None, :, None, :, None]
    full = jnp.where(eye, w4[:, :, :, None, :], 0.0)
    return full.reshape(8, 8 * a, 8 * b).astype(BF16)


def _s5_params(p, e):
    ldt = jnp.repeat(p['s5_log_dt'][e], S5_P).reshape(8, 512)
    return (p['s5_lambda_re'][e].reshape(8, 512), p['s5_lambda_im'][e].reshape(8, 512), ldt,
            _block_diag(p['s5_b_re'][e], True), _block_diag(p['s5_b_im'][e], True),
            _block_diag(p['s5_c_re'][e], True), _block_diag(p['s5_c_im'][e], True),
            p['s5_d'][e].reshape(1, W), p['s5_glu_w'][e].astype(BF16), p['s5_glu_b'][e].reshape(1, W))


SGU_R = 128


def _sgu_body(ub_ref, vb_ref, g_ref, b_ref, wp_ref, ws_ref, btp_ref, bts_ref, yb_ref, vn_ref):
    is_p = pl.program_id(0) < T_P // SGU_R
    v = jax.nn.gelu(vb_ref[...])
    u = jax.nn.gelu(ub_ref[...])
    mu = jnp.mean(v, axis=-1, keepdims=True)
    xc = v - mu
    var = jnp.mean(xc * xc, axis=-1, keepdims=True)
    vn = xc * lax.rsqrt(var + EPS) * g_ref[...] + b_ref[...]
    vn_ref[...] = vn
    row = lax.broadcasted_iota(jnp.int32, (SGU_R, SGU_R), 0)
    col = lax.broadcasted_iota(jnp.int32, (SGU_R, SGU_R), 1)
    shift = jnp.where(is_p, 31, 2)
    seg_r, seg_c = row >> shift, col >> shift
    mask = (seg_r == seg_c) & ((row - (seg_r << shift)) >= (col - (seg_c << shift)))
    vnb = vn.astype(BF16)
    outs = []
    for h in range(HEADS):
        blk = slice(HD * h, HD * (h + 1))
        w = jnp.where(is_p, wp_ref[h], ws_ref[h])
        wm = jnp.where(mask, w, 0.0).astype(BF16)
        bias = jnp.where(is_p, btp_ref[:, h:h + 1], bts_ref[:, h:h + 1])
        outs.append(u[:, blk] * (_dot(wm, vnb[:, blk]) + bias))
    yb_ref[...] = jnp.concatenate(outs, axis=1).astype(BF16)


def _sgu(z, p, e):
    w = p['sgu_w'][e]
    bt = p['sgu_b'][e].T
    w_s = jnp.tile(w[:, :L_S, :L_S], (1, SGU_R // L_S, SGU_R // L_S))
    bt_s = jnp.tile(bt[:L_S], (SGU_R // L_S, 1))
    npb = T_P // SGU_R
    full = lambda shape: pl.BlockSpec(shape, lambda i: (0,) * len(shape))
    return pl.pallas_call(
        _sgu_body,
        grid=(T // SGU_R,),
        in_specs=[pl.BlockSpec((SGU_R, W), lambda i: (i, 1)),
                  pl.BlockSpec((SGU_R, W), lambda i: (i, 2)),
                  full((1, W)), full((1, W)),
                  full((HEADS, SGU_R, SGU_R)), full((HEADS, SGU_R, SGU_R)),
                  full((SGU_R, HEADS)), full((SGU_R, HEADS))],
        out_specs=[pl.BlockSpec((SGU_R, W), lambda i: (i, 0)),
                   pl.BlockSpec((SGU_R, W), lambda i: (jnp.maximum(i - npb, 0), 0))],
        out_shape=[jax.ShapeDtypeStruct((T, W), BF16), jax.ShapeDtypeStruct((T_S, W), F32)],
        compiler_params=_cp(("arbitrary",)),
    )(z, z, p['sgu_norm_g'][e].reshape(1, W), p['sgu_norm_b'][e].reshape(1, W), w, w_s, bt, bt_s)


def _rglru_core(lseg, xr, shifted, xg, hinit, cw_ref, cb_ref, wa_ref, wx_ref, ba_ref, bx_ref, lam_ref):
    rows = xr.shape[0]
    xc = cb_ref[...]
    for j in range(CONV_W - 1):
        xc = xc + cw_ref[j:j + 1, :] * shifted[CONV_W - 1 - j]
    xc = xc + cw_ref[CONV_W - 1:CONV_W, :] * xr
    xcb = xc.astype(BF16)
    ra, ri = [], []
    for h in range(HEADS):
        blk = slice(HD * h, HD * (h + 1))
        ra.append(_dot(xcb[:, blk], wa_ref[h]))
        ri.append(_dot(xcb[:, blk], wx_ref[h]))
    r = jax.nn.sigmoid(jnp.concatenate(ra, axis=1) + ba_ref[...])
    gi = jax.nn.sigmoid(jnp.concatenate(ri, axis=1) + bx_ref[...])
    nl = -lam_ref[...]
    softplus = jnp.maximum(nl, 0.0) + jnp.log1p(jnp.exp(-jnp.abs(nl)))
    log_a = -RG_C * r * softplus
    a = jnp.exp(log_a)
    b = jnp.sqrt(1.0 - jnp.exp(2.0 * log_a)) * gi * xc
    pos = lax.broadcasted_iota(jnp.int32, (rows, W), 0) & (lseg - 1)
    d = 1
    while d < lseg:
        a_sh = pltpu.roll(a, d, 0)
        b_sh = pltpu.roll(b, d, 0)
        m = pos >= d
        b = jnp.where(m, a * b_sh + b, b)
        a = jnp.where(m, a * a_sh, a)
        d *= 2
    h = a * hinit + b
    return h, h * jax.nn.gelu(xg)


def _rglru_seq_body(lseg, xr_ref, xg_ref, cw_ref, cb_ref, wa_ref, wx_ref, ba_ref, bx_ref, lam_ref,
                    yc_ref, hl_ref, tail, hc):
    @pl.when(pl.program_id(1) == 0)
    def _():
        tail[...] = jnp.zeros_like(tail)
        hc[...] = jnp.zeros_like(hc)

    xr = xr_ref[...]
    row8 = lax.broadcasted_iota(jnp.int32, (8, W), 0)
    shifted = {}
    for d in range(1, CONV_W):
        sh = pltpu.roll(xr, d, 0)
        first = jnp.where(row8 >= d, sh[0:8], pltpu.roll(tail[...], d, 0))
        shifted[d] = jnp.concatenate([first, sh[8:]], axis=0)
    h, yc = _rglru_core(lseg, xr, shifted, xg_ref[...], hc[...], cw_ref, cb_ref, wa_ref, wx_ref,
                        ba_ref, bx_ref, lam_ref)
    yc_ref[...] = yc.astype(BF16)
    tail[...] = xr[lseg - 8:lseg]
    hc[...] = h[lseg - 1:lseg]
    hl_ref[...] = h[lseg - 1:lseg]


def _rglru_dec_body(lseg, xr_ref, xg_ref, halo1_ref, halo2_ref, halo3_ref, h0_ref, cw_ref, cb_ref,
                    wa_ref, wx_ref, ba_ref, bx_ref, lam_ref, yc_ref, hs_ref):
    xr = xr_ref[...]
    pos = lax.broadcasted_iota(jnp.int32, xr.shape, 0) & (lseg - 1)
    halos = {1: halo1_ref, 2: halo2_ref, 3: halo3_ref}
    shifted = {d: jnp.where(pos >= d, pltpu.roll(xr, d, 0), halos[d][...]) for d in range(1, CONV_W)}
    h, yc = _rglru_core(lseg, xr, shifted, xg_ref[...], h0_ref[...], cw_ref, cb_ref, wa_ref, wx_ref,
                        ba_ref, bx_ref, lam_ref)
    yc_ref[...] = yc.astype(BF16)
    hs_ref[...] = h


def _rglru_params(p, o):
    return (p['rglru_conv_w'][o], p['rglru_conv_b'][o].reshape(1, W),
            p['rglru_wa'][o].astype(BF16), p['rglru_wx'][o].astype(BF16),
            p['rglru_ba'][o].reshape(1, W), p['rglru_bx'][o].reshape(1, W),
            p['rglru_lambda'][o].reshape(1, W))


def _rglru_param_specs(nd):
    full = lambda shape: pl.BlockSpec(shape, lambda *g: (0,) * len(shape))
    return [full((CONV_W, W)), full((1, W)), full((HEADS, HD, HD)), full((HEADS, HD, HD)),
            full((1, W)), full((1, W)), full((1, W))]


def _rglru_prompt(z, prm):
    lseg = 256
    nch = L_P // lseg
    return pl.pallas_call(
        functools.partial(_rglru_seq_body, lseg),
        grid=(B_P, nch),
        in_specs=[pl.BlockSpec((lseg, W), lambda b, c: (b * nch + c, 0)),
                  pl.BlockSpec((lseg, W), lambda b, c: (b * nch + c, 1))] + _rglru_param_specs(2),
        out_specs=[pl.BlockSpec((lseg, W), lambda b, c: (b * nch + c, 0)),
                   pl.BlockSpec((None, 1, W), lambda b, c: (b, 0, 0))],
        out_shape=[jax.ShapeDtypeStruct((T_P, W), BF16), jax.ShapeDtypeStruct((B_P, 1, W), F32)],
        scratch_shapes=[pltpu.VMEM((8, W), F32), pltpu.VMEM((1, W), F32)],
        compiler_params=_cp(("arbitrary", "arbitrary")),
    )(z, z, *prm)


def _rglru_sample(z, conv0, h0, prm):
    rows = 128
    blk0 = T_P // rows

    def halo(d):
        return jnp.pad(conv0[:, CONV_W - 1 - d:, :], ((0, 0), (0, L_S - d), (0, 0))).reshape(T_S, W)

    row_spec = pl.BlockSpec((rows, W), lambda i: (i, 0))
    return pl.pallas_call(
        functools.partial(_rglru_dec_body, L_S),
        grid=(T_S // rows,),
        in_specs=[pl.BlockSpec((rows, W), lambda i: (blk0 + i, 0)),
                  pl.BlockSpec((rows, W), lambda i: (blk0 + i, 1)),
                  row_spec, row_spec, row_spec, row_spec] + _rglru_param_specs(1),
        out_specs=[row_spec, row_spec],
        out_shape=[jax.ShapeDtypeStruct((T_S, W), BF16), jax.ShapeDtypeStruct((T_S, W), F32)],
        compiler_params=_cp(("arbitrary",)),
    )(z, z, halo(1), halo(2), halo(3), jnp.repeat(h0, L_S, axis=0), *prm)


def _hgrn_tables(rows, lseg):
    t = np.arange(rows)[:, None]
    j = np.arange(rows)[None, :]
    same_seg = (t // lseg) == (j // lseg)
    mats = [same_seg & (j <= t), same_seg & (j > t)]
    masks = [t == j]
    hs = lseg // 2
    while hs >= 1:
        same_blk = (t // (2 * hs)) == (j // (2 * hs))
        up_t = (t % (2 * hs)) >= hs
        up_j = (j % (2 * hs)) >= hs
        mats.append(same_blk & up_t & up_j & (j <= t))
        mats.append(same_blk & ~up_t & ~up_j & (j > t))
        masks.append(same_blk & up_t & ~up_j)
        hs //= 2
    return (np.concatenate(mats, axis=0).astype(np.float32), np.stack(masks).astype(np.float32))


def _hgrn_body(lseg, nseg, carried, q_ref, f_ref, v_ref, g_ref, lb_ref, ng_ref, lmat_ref, mask_ref,
               *rest):
    if carried:
        od_ref, so_ref, s_scr, o_scr = rest
        s_in = s_scr

        @pl.when(pl.program_id(1) == 0)
        def _():
            s_scr[...] = jnp.zeros_like(s_scr)
    else:
        s0_ref, od_ref, so_ref, o_scr = rest
        s_in = s0_ref
    rows = lseg * nseg
    nlev = mask_ref.shape[0] - 1

    lbp = lb_ref[...]
    mx = jnp.max(lbp, axis=0, keepdims=True)
    ex = jnp.exp(lbp - mx)
    sm = ex / jnp.sum(ex, axis=0, keepdims=True)
    lb = (sm[0:1] + sm[1:2]) - sm[0:1]
    fg = lb + (1.0 - lb) * jax.nn.sigmoid(f_ref[...])
    logf = jnp.log(fg)
    kk = 1.0 - fg
    expo = _dot_exact(lmat_ref[...], logf)
    q = q_ref[...]
    v = v_ref[...]
    seg_id = lax.broadcasted_iota(jnp.int32, (rows, HD), 0) >> (lseg.bit_length() - 1)
    ones = jnp.ones((rows, HD), F32)

    for h in range(HEADS):
        blk = slice(HD * h, HD * (h + 1))
        qh, kh = q[:, blk], kk[:, blk]
        vb = v[:, blk].astype(BF16)
        tab = lambda n: expo[n * rows:(n + 1) * rows, blk]
        att = mask_ref[0] * _dot_nt(qh.astype(BF16), kh.astype(BF16))
        for lv in range(nlev):
            ql = (qh * jnp.exp(tab(2 + 2 * lv))).astype(BF16)
            kl = (kh * jnp.exp(tab(3 + 2 * lv))).astype(BF16)
            att = att + mask_ref[1 + lv] * _dot_nt(ql, kl)
        o = _dot(att.astype(BF16), vb)
        qg = qh * jnp.exp(tab(0))
        kdec = kh * jnp.exp(tab(1))
        lf = logf[:, blk]
        for j in range(nseg):
            if nseg > 1:
                m = seg_id == j
                qg_j = jnp.where(m, qg, 0.0)
                kdec_j = jnp.where(m, kdec, 0.0)
                lf_j = jnp.where(m, lf, 0.0)
            else:
                qg_j, kdec_j, lf_j = qg, kdec, lf
            s_old = s_in[j, h]
            o = o + _dot(qg_j.astype(BF16), s_old.astype(BF16))
            total = _dot_tn(lf_j, ones, precision=HIGHEST)
            s_new = jnp.exp(total) * s_old + _dot_tn(kdec_j.astype(BF16), vb)
            so_ref[j, h] = s_new
            if carried:
                s_scr[j, h] = s_new
        o_scr[:, blk] = o * lax.rsqrt(jnp.mean(o * o, axis=-1, keepdims=True) + EPS)
    gate = g_ref[...]
    od_ref[...] = (o_scr[...] * ng_ref[...] * (gate * jax.nn.sigmoid(gate))).astype(BF16)


def _hgrn_prompt(z, lb, ng):
    lseg = 64
    nch = L_P // lseg
    lmat, masks = _hgrn_tables(lseg, lseg)
    zspec = lambda col: pl.BlockSpec((lseg, W), lambda b, c: (b * nch + c, col))
    full = lambda shape: pl.BlockSpec(shape, lambda b, c: (0,) * len(shape))
    return pl.pallas_call(
        functools.partial(_hgrn_body, lseg, 1, True),
        grid=(B_P, nch),
        in_specs=[zspec(2), zspec(3), zspec(4), zspec(5), full((2, W)), full((1, W)),
                  full(lmat.shape), full(masks.shape)],
        out_specs=[pl.BlockSpec((lseg, W), lambda b, c: (b * nch + c, 0)),
                   pl.BlockSpec((1, HEADS, HD, HD), lambda b, c: (b, 0, 0, 0))],
        out_shape=[jax.ShapeDtypeStruct((T_P, W), BF16),
                   jax.ShapeDtypeStruct((B_P, HEADS, HD, HD), F32)],
        scratch_shapes=[pltpu.VMEM((1, HEADS, HD, HD), F32), pltpu.VMEM((lseg, W), F32)],
        compiler_params=_cp(("arbitrary", "arbitrary")),
    )(z, z, z, z, lb, ng, jnp.asarray(lmat), jnp.asarray(masks))


def _hgrn_sample(z, s0, lb, ng):
    nseg = 8
    rows = nseg * L_S
    blk0 = T_P // rows
    lmat, masks = _hgrn_tables(rows, L_S)
    zspec = lambda col: pl.BlockSpec((rows, W), lambda i: (blk0 + i, col))
    full = lambda shape: pl.BlockSpec(shape, lambda i: (0,) * len(shape))
    sspec = pl.BlockSpec((nseg, HEADS, HD, HD), lambda i: (i, 0, 0, 0))
    return pl.pallas_call(
        functools.partial(_hgrn_body, L_S, nseg, False),
        grid=(T_S // rows,),
        in_specs=[zspec(2), zspec(3), zspec(4), zspec(5), full((2, W)), full((1, W)),
                  full(lmat.shape), full(masks.shape), sspec],
        out_specs=[pl.BlockSpec((rows, W), lambda i: (i, 0)), sspec],
        out_shape=[jax.ShapeDtypeStruct((T_S, W), BF16),
                   jax.ShapeDtypeStruct((B_S, HEADS, HD, HD), F32)],
        scratch_shapes=[pltpu.VMEM((rows, W), F32)],
        compiler_params=_cp(("arbitrary",)),
    )(z, z, z, z, lb, ng, jnp.asarray(lmat), jnp.asarray(masks), s0)


def _router_body(h_ref, w_ref, b_ref, idx_ref, comb_ref):
    logits = _dot_exact(h_ref[...], w_ref[...]) + b_ref[...]
    col = lax.broadcasted_iota(jnp.int32, logits.shape, 1).astype(F32)
    neg = jnp.float32(-jnp.inf)
    big = jnp.float32(1e9)

    def masked_softmax(mask):
        lm = jnp.where(mask, logits, neg)
        e = jnp.where(mask, jnp.exp(lm - jnp.max(lm, axis=-1, keepdims=True)), 0.0)
        return e / jnp.sum(e, axis=-1, keepdims=True)

    def top1(pv, mask):
        pm = jnp.where(mask, pv, -1.0)
        best = jnp.max(pm, axis=-1, keepdims=True)
        first = jnp.min(jnp.where(pm == best, col, big), axis=-1, keepdims=True)
        return best, first

    gmask = col < N_GRP
    gw, gi = top1(masked_softmax(gmask), gmask)
    lo = N_GRP + PER_GRP * gi
    emask = (col >= lo) & (col < lo + PER_GRP)
    pe = masked_softmax(emask)
    w1, i1 = top1(pe, emask)
    w2, i2 = top1(pe, emask & (col != i1))
    den = w1 + w2
    idx_ref[...] = jnp.where(col == 0, i1 - N_GRP, jnp.where(col == 1, i2 - N_GRP, 0.0)).astype(jnp.int32)
    comb_ref[...] = jnp.where(col == 0, gw * (w1 / den), jnp.where(col == 1, gw * (w2 / den), 0.0))


def _router(hf, p, l):
    wcat = jnp.pad(jnp.concatenate([p['moe_w_grp'][l], p['moe_w_exp'][l]], axis=1),
                   ((0, 0), (0, HD - N_GRP - N_EXP)))
    bcat = jnp.pad(jnp.concatenate([p['moe_b_grp'][l], p['moe_b_exp'][l]]),
                   (0, HD - N_GRP - N_EXP)).reshape(1, HD)
    return pl.pallas_call(
        _router_body,
        grid=(T // TM,),
        in_specs=[pl.BlockSpec((TM, D), lambda i: (i, 0)),
                  pl.BlockSpec((D, HD), lambda i: (0, 0)),
                  pl.BlockSpec((1, HD), lambda i: (0, 0))],
        out_specs=[pl.BlockSpec((TM, HD), lambda i: (i, 0)), pl.BlockSpec((TM, HD), lambda i: (i, 0))],
        out_shape=[jax.ShapeDtypeStruct((T, HD), jnp.int32), jax.ShapeDtypeStruct((T, HD), F32)],
        compiler_params=_cp(("arbitrary",)),
    )(hf, wcat, bcat)


def _dispatch(eidx):
    e_flat = eidx.reshape(-1)
    onehot = (e_flat[:, None] == jnp.arange(N_EXP, dtype=jnp.int32)[None, :]).astype(jnp.int32)
    rank = jnp.sum((jnp.cumsum(onehot, axis=0) - onehot) * onehot, axis=1)
    counts = jnp.sum(onehot, axis=0)
    padded = ((counts + TR - 1) // TR) * TR
    ends = jnp.cumsum(padded)
    starts = ends - padded
    slot = starts[e_flat] + rank
    src_tok = jnp.zeros((R_TOT,), jnp.int32).at[slot].set(jnp.arange(2 * T, dtype=jnp.int32) // 2)
    n_used = (ends[-1] // TR).astype(jnp.int32)
    tile_start = jnp.arange(N_RT, dtype=jnp.int32) * TR
    tile_exp = jnp.sum((tile_start[:, None] >= ends[None, :]).astype(jnp.int32), axis=1)
    last_exp = jnp.sum((jnp.maximum(ends[-1] - 1, 0) >= ends).astype(jnp.int32))
    tile_exp = jnp.minimum(tile_exp, last_exp).astype(jnp.int32)
    return slot.astype(jnp.int32), src_tok, tile_exp, n_used.reshape(1)


def _row_copy(src_hbm, row, dst_ref, r, sem):
    return pltpu.make_async_copy(src_hbm.at[pl.ds(row, 1), :], dst_ref.at[pl.ds(r, 1), :], sem)


def _gather_body(src_ref, nused_ref, h_hbm, o_ref, sem):
    i = pl.program_id(0)

    @pl.when(i < nused_ref[0])
    def _():
        def issue(r, c):
            _row_copy(h_hbm, src_ref[i * TR + r], o_ref, r, sem).start()
            return c

        lax.fori_loop(0, TR, issue, 0)

        def drain(r, c):
            _row_copy(h_hbm, 0, o_ref, r, sem).wait()
            return c

        lax.fori_loop(0, TR, drain, 0)

    @pl.when(i >= nused_ref[0])
    def _():
        o_ref[...] = jnp.zeros_like(o_ref)


def _gather(hf, src_tok, n_used):
    return pl.pallas_call(
        _gather_body,
        grid_spec=pltpu.PrefetchScalarGridSpec(
            num_scalar_prefetch=2, grid=(N_RT,),
            in_specs=[pl.BlockSpec(memory_space=pl.ANY)],
            out_specs=pl.BlockSpec((TR, D), lambda i, s, n: (i, 0)),
            scratch_shapes=[pltpu.SemaphoreType.DMA(())]),
        out_shape=jax.ShapeDtypeStruct((R_TOT, D), F32),
        compiler_params=_cp(("arbitrary",)),
    )(src_tok, n_used, hf)


def _experts_body(te_ref, nused_ref, x_ref, wg_ref, wu_ref, wd_ref, o_ref, wgb, wub, wdb):
    i = pl.program_id(0)
    prev = te_ref[jnp.maximum(i - 1, 0)]
    used = i < nused_ref[0]

    @pl.when(used & ((i == 0) | (te_ref[i] != prev)))
    def _():
        wgb[...] = wg_ref[...].astype(BF16)
        wub[...] = wu_ref[...].astype(BF16)
        wdb[...] = wd_ref[...].astype(BF16)

    @pl.when(used)
    def _():
        xb = x_ref[...].astype(BF16)
        g = _dot(xb, wgb[...])
        u = _dot(xb, wub[...])
        hid = (g * jax.nn.sigmoid(g)) * u
        o_ref[...] = _dot(hid.astype(BF16), wdb[...])

    @pl.when(jnp.logical_not(used))
    def _():
        o_ref[...] = jnp.zeros_like(o_ref)


def _experts(xs, tile_exp, n_used, p, l):
    e0 = l * N_EXP
    wg = p['moe_w_gate'].reshape(-1, D, D_EXP)
    wu = p['moe_w_up'].reshape(-1, D, D_EXP)
    wd = p['moe_w_down'].reshape(-1, D_EXP, D)
    return pl.pallas_call(
        _experts_body,
        grid_spec=pltpu.PrefetchScalarGridSpec(
            num_scalar_prefetch=2, grid=(N_RT,),
            in_specs=[pl.BlockSpec((TR, D), lambda i, te, n: (i, 0)),
                      pl.BlockSpec((None, D, D_EXP), lambda i, te, n: (e0 + te[i], 0, 0)),
                      pl.BlockSpec((None, D, D_EXP), lambda i, te, n: (e0 + te[i], 0, 0)),
                      pl.BlockSpec((None, D_EXP, D), lambda i, te, n: (e0 + te[i], 0, 0))],
            out_specs=pl.BlockSpec((TR, D), lambda i, te, n: (i, 0)),
            scratch_shapes=[pltpu.VMEM((D, D_EXP), BF16), pltpu.VMEM((D, D_EXP), BF16),
                            pltpu.VMEM((D_EXP, D), BF16)]),
        out_shape=jax.ShapeDtypeStruct((R_TOT, D), F32),
        compiler_params=_cp(("arbitrary",)),
    )(tile_exp, n_used, xs, wg, wu, wd)


def _combine_body(slot_ref, y_hbm, x_ref, comb_ref, gp_ref, gs_ref, o_ref, y0, y1, sem):
    i = pl.program_id(0)

    def issue(r, c):
        base = 2 * (i * TM + r)
        _row_copy(y_hbm, slot_ref[base], y0, r, sem).start()
        _row_copy(y_hbm, slot_ref[base + 1], y1, r, sem).start()
        return c

    lax.fori_loop(0, TM, issue, 0)

    def drain(r, c):
        _row_copy(y_hbm, 0, y0, r, sem).wait()
        _row_copy(y_hbm, 0, y1, r, sem).wait()
        return c

    lax.fori_loop(0, TM, drain, 0)
    comb = comb_ref[...]
    moe = comb[:, 0:1] * y0[...] + comb[:, 1:2] * y1[...]
    gate = _mod_pick(i < T_P // TM, gp_ref, gs_ref)
    o_ref[...] = x_ref[...] + gate * moe


def _combine(ys, slot, x, comb, mod_p, mod_s, gate_chunk):
    return pl.pallas_call(
        _combine_body,
        grid_spec=pltpu.PrefetchScalarGridSpec(
            num_scalar_prefetch=1, grid=(T // TM,),
            in_specs=[pl.BlockSpec(memory_space=pl.ANY),
                      pl.BlockSpec((TM, D), lambda i, s: (i, 0)),
                      pl.BlockSpec((TM, HD), lambda i, s: (i, 0))] + _mod_specs(TM, gate_chunk),
            out_specs=pl.BlockSpec((TM, D), lambda i, s: (i, 0)),
            scratch_shapes=[pltpu.VMEM((TM, D), F32), pltpu.VMEM((TM, D), F32),
                            pltpu.SemaphoreType.DMA(())]),
        out_shape=jax.ShapeDtypeStruct((T, D), F32),
        compiler_params=_cp(("arbitrary",)),
    )(slot, ys, x, comb, mod_p, mod_s)


def _moe(hf, x, p, l, mod_p, mod_s):
    eidx, comb = _router(hf, p, l)
    slot, src_tok, tile_exp, n_used = _dispatch(eidx[:, :2])
    xs = _gather(hf, src_tok, n_used)
    ys = _experts(xs, tile_exp, n_used, p, l)
    return _combine(ys, slot, x, comb, mod_p, mod_s, 5)


def kernel(x_prompt, x_sample, c_prompt, c_sample, state_s5_re, state_s5_im, state_rglru_h, state_rglru_conv, state_hgrn2, ada_w, ada_b, norm_mix_g, norm_ffn_g, norm_final_g, w_in_ab, w_out_ab, s5_lambda_re, s5_lambda_im, s5_log_dt, s5_b_re, s5_b_im, s5_c_re, s5_c_im, s5_d, s5_glu_w, s5_glu_b, sgu_norm_g, sgu_norm_b, sgu_w, sgu_b, w_in_cd, w_out_cd, rglru_conv_w, rglru_conv_b, rglru_wa, rglru_ba, rglru_wx, rglru_bx, rglru_lambda, hgrn_lb, hgrn_norm_g, moe_w_grp, moe_b_grp, moe_w_exp, moe_b_exp, moe_w_gate, moe_w_up, moe_w_down):
    p = dict(s5_lambda_re=s5_lambda_re, s5_lambda_im=s5_lambda_im, s5_log_dt=s5_log_dt,
             s5_b_re=s5_b_re, s5_b_im=s5_b_im, s5_c_re=s5_c_re, s5_c_im=s5_c_im, s5_d=s5_d,
             s5_glu_w=s5_glu_w, s5_glu_b=s5_glu_b, sgu_norm_g=sgu_norm_g, sgu_norm_b=sgu_norm_b,
             sgu_w=sgu_w, sgu_b=sgu_b, rglru_conv_w=rglru_conv_w, rglru_conv_b=rglru_conv_b,
             rglru_wa=rglru_wa, rglru_ba=rglru_ba, rglru_wx=rglru_wx, rglru_bx=rglru_bx,
             rglru_lambda=rglru_lambda, moe_w_grp=moe_w_grp, moe_b_grp=moe_b_grp,
             moe_w_exp=moe_w_exp, moe_b_exp=moe_b_exp, moe_w_gate=moe_w_gate, moe_w_up=moe_w_up,
             moe_w_down=moe_w_down)
    n_cond = B_P + B_S
    c_all = jnp.concatenate([c_prompt, c_sample, jnp.zeros((8 - n_cond % 8, D), F32)], axis=0)
    mod = _adaln(c_all, ada_w, ada_b)
    x = jnp.concatenate([x_prompt.reshape(T_P, D), x_sample.reshape(T_S, D)], axis=0)
    outs = {}
    for l in range(2):
        mod_p = mod[l, :B_P].reshape(B_P, 1, 6 * D)
        mod_s = jnp.repeat(mod[l, B_P:n_cond], L_S, axis=0)
        hmix = _normmod(x, norm_mix_g[l], mod_p, mod_s, 1, 0, BF16)
        if l == 0:
            z = _mm_in(hmix, w_in_ab, 0)
            prm = _s5_params(p, 0)
            zeros_p = jnp.zeros((B_P, 8, 512), F32)
            ya_p, hr_p, hi_p = _s5(z, 0, B_P, L_P, zeros_p, zeros_p, prm)
            ya_s, hr_s, hi_s = _s5(z, T_P, B_S, L_S, state_s5_re[0].reshape(B_S, 8, 512),
                                   state_s5_im[0].reshape(B_S, 8, 512), prm)
            yb, vn_s = _sgu(z, p, 0)
            x = _mm_out(jnp.concatenate([ya_p, ya_s], axis=0), yb, w_out_ab, 0, x, mod_p, mod_s, 2)
            outs['s5'] = (hr_p, hi_p, hr_s, hi_s)
            outs['sgu_v'] = vn_s
        else:
            z = _mm_in(hmix, w_in_cd, 0)
            prm = _rglru_params(p, 0)
            yc_p, hl_p = _rglru_prompt(z, prm)
            yc_s, hseq_s = _rglru_sample(z, state_rglru_conv[0], state_rglru_h[0], prm)
            ng = hgrn_norm_g[0].reshape(1, W)
            od_p, s_p = _hgrn_prompt(z, hgrn_lb, ng)
            od_s, s_s = _hgrn_sample(z, state_hgrn2[0], hgrn_lb, ng)
            x = _mm_out(jnp.concatenate([yc_p, yc_s], axis=0), jnp.concatenate([od_p, od_s], axis=0),
                        w_out_cd, 0, x, mod_p, mod_s, 2)
            xr_p = z[:T_P, :W].reshape(B_P, L_P, W)
            xr_s = z[T_P:, :W].reshape(B_S, L_S, W)
            outs['rg'] = (hl_p.reshape(1, B_P, W), hseq_s.reshape(B_S, L_S, W)[:, -1][None],
                          xr_p[:, L_P - (CONV_W - 1):][None], xr_s[:, L_S - (CONV_W - 1):][None])
            outs['hg'] = (s_p[None], s_s[None])
        hf = _normmod(x, norm_ffn_g[l], mod_p, mod_s, 4, 3, F32)
        x = _moe(hf, x, p, l, mod_p, mod_s)
    y = _finalnorm(x, norm_final_g)
    hr_p, hi_p, hr_s, hi_s = outs['s5']
    st = lambda a, b: a.reshape(1, b, S5_G, S5_P)
    rgh_p, rgh_s, rgc_p, rgc_s = outs['rg']
    return (y[:T_P].reshape(B_P, L_P, D), y[T_P:].reshape(B_S, L_S, D),
            st(hr_p, B_P), st(hi_p, B_P), st(hr_s, B_S), st(hi_s, B_S),
            outs['sgu_v'].reshape(1, B_S, L_S, W),
            rgh_p, rgh_s, rgc_p, rgc_s, outs['hg'][0], outs['hg'][1])
```
